```python
import jax, jax.numpy as jnp
from jax import lax
import numpy as np

D_MODEL = 2048
BATCH = 32
SEQ = 256
DEPTH = 4
DEC_BATCH = 8
DEC_SEQ = 2048
PAST_LEN = 256

GRID_W = 64
HEAD_DIM = 128
A_HEADS = D_MODEL // (2 * HEAD_DIM)
A_KV_HEADS = A_HEADS // 4
A_GROUPS = A_HEADS // A_KV_HEADS
A_WINDOW = 128
A_BLOCK = 128
B_HEADS = D_MODEL // (2 * HEAD_DIM)
NB_KH = 8
NB_KW = 16
A_WIDTH = A_HEADS * HEAD_DIM
A_KV_WIDTH = A_KV_HEADS * HEAD_DIM
B_WIDTH = B_HEADS * HEAD_DIM
ATTN_WIDTH = A_WIDTH + B_WIDTH
ATTN_SPLITS = (A_WIDTH, A_WIDTH + A_KV_WIDTH, A_WIDTH + 2 * A_KV_WIDTH,
               A_WIDTH + 2 * A_KV_WIDTH + B_WIDTH, A_WIDTH + 2 * A_KV_WIDTH + 2 * B_WIDTH,
               A_WIDTH + 2 * A_KV_WIDTH + 3 * B_WIDTH)
ATTN_IN = A_WIDTH + 2 * A_KV_WIDTH + 3 * B_WIDTH + ATTN_WIDTH
CONV_WIDTH = D_MODEL
CONV_K = 31
N_ATTN_LAYERS = (DEPTH + 1) // 2
N_CONV_LAYERS = DEPTH // 2
CTX_QBLOCK = 128
ROPE_BASE = 10000.0
EPS = 1e-6

kernel_name = "hybrid_flow_prefix_trunk_step"


def rms_norm(x, g):
    xf = x.astype(jnp.float32)
    y = xf * lax.rsqrt(jnp.mean(xf * xf, axis=-1, keepdims=True) + EPS)
    return (y * g.astype(jnp.float32)).astype(x.dtype)


def layer_norm(x, g, b):
    xf = x.astype(jnp.float32)
    mu = jnp.mean(xf, axis=-1, keepdims=True)
    xc = xf - mu
    var = jnp.mean(xc * xc, axis=-1, keepdims=True)
    return (xc * lax.rsqrt(var + EPS) * g.astype(jnp.float32) + b.astype(jnp.float32)).astype(x.dtype)


def modulation(cond, w, b):
    m = (jax.nn.silu(cond) @ w + b)[:, None, :]
    return jnp.split(m, 3, axis=-1)


def axial_rope_tables(length, dtype):
    t = jnp.arange(length)
    row = (t // GRID_W).astype(jnp.float32)
    col = (t % GRID_W).astype(jnp.float32)
    n_freq = HEAD_DIM // 4
    inv = ROPE_BASE ** (-jnp.arange(n_freq, dtype=jnp.float32) / n_freq)
    ang_r = row[:, None] * inv[None, :]
    ang_c = col[:, None] * inv[None, :]
    ang = jnp.concatenate([ang_r, ang_r, ang_c, ang_c], axis=-1)
    return jnp.cos(ang).astype(dtype), jnp.sin(ang).astype(dtype)


def apply_rope(x, cos, sin):
    x1r, x2r, x1c, x2c = jnp.split(x, 4, axis=-1)
    rot = jnp.concatenate([-x2r, x1r, -x2c, x1c], axis=-1)
    return x * cos[None, :, None, :] + rot * sin[None, :, None, :]


def softmax_with_sink(s, sink):
    m = jnp.maximum(jnp.max(s, axis=-1, keepdims=True), sink)
    p = jnp.exp(s - m)
    return p / (jnp.sum(p, axis=-1, keepdims=True) + jnp.exp(sink - m))


def context_attention(q, k, v, sink):
    bn, lc, kvh, grp, hd = q.shape
    nq = lc // CTX_QBLOCK
    qb = jnp.moveaxis(q.reshape(bn, nq, CTX_QBLOCK, kvh, grp, hd), 1, 0)
    scale = HEAD_DIM ** -0.5

    def block(qq):
        s = jnp.einsum('bqkgd,bskd->bkgqs', qq, k, preferred_element_type=jnp.float32) * scale
        if sink is None:
            p = jax.nn.softmax(s, axis=-1)
        else:
            p = softmax_with_sink(s, sink.astype(jnp.float32)[None, :, :, None, None])
        return jnp.einsum('bkgqs,bskd->bqkgd', p.astype(v.dtype), v)

    o = lax.map(block, qb)
    return jnp.moveaxis(o, 0, 1).reshape(bn, lc, kvh * grp * hd)


def windowed_gqa_sink(q, k, v, k_ctx, v_ctx, sink):
    bn, length, _, hd = q.shape
    nb = length // A_BLOCK
    scale = HEAD_DIM ** -0.5
    qb = q.reshape(bn, nb, A_BLOCK, A_KV_HEADS, A_GROUPS, hd)

    def band(t):
        tp = jnp.pad(t, ((0, 0), (A_BLOCK, A_BLOCK), (0, 0), (0, 0)))
        tp = tp.reshape(bn, nb + 2, A_BLOCK, A_KV_HEADS, hd)
        return jnp.concatenate([tp[:, :-2], tp[:, 1:-1], tp[:, 2:]], axis=2)

    kw, vw = band(k), band(v)
    blk = jnp.arange(nb)[:, None, None]
    q_pos = blk * A_BLOCK + jnp.arange(A_BLOCK)[None, :, None]
    k_pos = blk * A_BLOCK - A_BLOCK + jnp.arange(3 * A_BLOCK)[None, None, :]
    mask = (jnp.abs(k_pos - q_pos) <= A_WINDOW) & (k_pos >= 0) & (k_pos < length)
    s_loc = jnp.einsum('bnqkgd,bnskd->bnkgqs', qb, kw, preferred_element_type=jnp.float32) * scale
    s_loc = jnp.where(mask[None, :, None, None], s_loc, -jnp.inf)
    s_ctx = jnp.einsum('bnqkgd,bskd->bnkgqs', qb, k_ctx, preferred_element_type=jnp.float32) * scale
    sink_b = sink.astype(jnp.float32).reshape(A_KV_HEADS, A_GROUPS)[None, None, :, :, None, None]
    p = softmax_with_sink(jnp.concatenate([s_loc, s_ctx], axis=-1), sink_b).astype(v.dtype)
    n_loc = 3 * A_BLOCK
    o = (jnp.einsum('bnkgqs,bnskd->bnqkgd', p[..., :n_loc], vw)
         + jnp.einsum('bnkgqs,bskd->bnqkgd', p[..., n_loc:], v_ctx))
    return o.reshape(bn, length, A_WIDTH)


def neighbourhood_attention(q, k, v, k_ctx, v_ctx, rpb):
    bn, length, nh, hd = q.shape
    rows = length // GRID_W
    wr = min(NB_KH, rows)
    scale = HEAD_DIM ** -0.5
    qg = jnp.moveaxis(q.reshape(bn, rows, GRID_W, nh, hd), 1, 0)
    kg = k.reshape(bn, rows, GRID_W, nh, hd)
    vg = v.reshape(bn, rows, GRID_W, nh, hd)
    cols = jnp.arange(GRID_W)
    col_start = jnp.clip(cols - NB_KW // 2, 0, GRID_W - NB_KW)
    col_ok = (cols[None, :] >= col_start[:, None]) & (cols[None, :] < col_start[:, None] + NB_KW)
    col_idx = jnp.clip(cols[None, :] - cols[:, None] + NB_KW - 1, 0, 2 * NB_KW - 2)
    col_mask = jnp.broadcast_to(col_ok[:, None, :], (GRID_W, wr, GRID_W)).reshape(GRID_W, wr * GRID_W)
    n_loc = wr * GRID_W

    def row_block(args):
        r, q_r = args
        r_start = jnp.clip(r - NB_KH // 2, 0, rows - wr)
        k_r = lax.dynamic_slice_in_dim(kg, r_start, wr, axis=1).reshape(bn, n_loc, nh, hd)
        v_r = lax.dynamic_slice_in_dim(vg, r_start, wr, axis=1).reshape(bn, n_loc, nh, hd)
        row_idx = r_start + jnp.arange(wr) - r + NB_KH - 1
        bias = rpb[:, row_idx[None, :, None], col_idx[:, None, :]]
        bias = bias.reshape(nh, GRID_W, n_loc).astype(jnp.float32)
        s_loc = jnp.einsum('bqhd,bshd->bhqs', q_r, k_r, preferred_element_type=jnp.float32) * scale + bias[None]
        s_loc = jnp.where(col_mask[None, None], s_loc, -jnp.inf)
        s_ctx = jnp.einsum('bqhd,bshd->bhqs', q_r, k_ctx, preferred_element_type=jnp.float32) * scale
        p = jax.nn.softmax(jnp.concatenate([s_loc, s_ctx], axis=-1), axis=-1).astype(v.dtype)
        return (jnp.einsum('bhqs,bshd->bqhd', p[..., :n_loc], v_r)
                + jnp.einsum('bhqs,bshd->bqhd', p[..., n_loc:], v_ctx))

    o = lax.map(row_block, (jnp.arange(rows), qg))
    return jnp.moveaxis(o, 0, 1).reshape(bn, length, nh * hd)


def attn_mixer_context(h, w_in, w_out, sink):
    bn, lc, _ = h.shape
    qa, ka, va, qb, kb, vb, z = jnp.split(h @ w_in, ATTN_SPLITS, axis=-1)
    ka = ka.reshape(bn, lc, A_KV_HEADS, HEAD_DIM)
    va = va.reshape(bn, lc, A_KV_HEADS, HEAD_DIM)
    kb = kb.reshape(bn, lc, B_HEADS, HEAD_DIM)
    vb = vb.reshape(bn, lc, B_HEADS, HEAD_DIM)
    oa = context_attention(qa.reshape(bn, lc, A_KV_HEADS, A_GROUPS, HEAD_DIM), ka, va,
                           sink.reshape(A_KV_HEADS, A_GROUPS))
    ob = context_attention(qb.reshape(bn, lc, B_HEADS, 1, HEAD_DIM), kb, vb, None)
    y = jnp.concatenate([oa, ob], axis=-1) * jax.nn.silu(z)
    return y @ w_out, ka, va, kb, vb


def attn_mixer_latent(h, k_ctx_a, v_ctx_a, k_ctx_b, v_ctx_b, w_in, w_out, sink, rpb, cos, sin):
    bn, length, _ = h.shape
    qa, ka, va, qb, kb, vb, z = jnp.split(h @ w_in, ATTN_SPLITS, axis=-1)
    qa = apply_rope(qa.reshape(bn, length, A_HEADS, HEAD_DIM), cos, sin)
    ka = apply_rope(ka.reshape(bn, length, A_KV_HEADS, HEAD_DIM), cos, sin)
    va = va.reshape(bn, length, A_KV_HEADS, HEAD_DIM)
    oa = windowed_gqa_sink(qa, ka, va, k_ctx_a, v_ctx_a, sink)
    ob = neighbourhood_attention(qb.reshape(bn, length, B_HEADS, HEAD_DIM),
                                 kb.reshape(bn, length, B_HEADS, HEAD_DIM),
                                 vb.reshape(bn, length, B_HEADS, HEAD_DIM), k_ctx_b, v_ctx_b, rpb)
    y = jnp.concatenate([oa, ob], axis=-1) * jax.nn.silu(z)
    return y @ w_out


def conv_mixer(h, w_in, dw, dw_b, ln_g, ln_b, w_out):
    a, g, z = jnp.split(h @ w_in, 3, axis=-1)
    u = a * jax.nn.sigmoid(g)
    u = lax.conv_general_dilated(u, dw[:, None, :], window_strides=(1,),
                                 padding=[(CONV_K // 2, CONV_K // 2)],
                                 dimension_numbers=('NWC', 'WIO', 'NWC'),
                                 feature_group_count=CONV_WIDTH) + dw_b
    y = jax.nn.silu(layer_norm(u, ln_g, ln_b)) * jax.nn.silu(z)
    return y @ w_out


def setup_inputs(seed: int = 0) -> dict:
    key = jax.random.key(seed)
    ks = jax.random.split(key, 22)

    def nrm(k, shape, s):
        return jax.random.normal(k, shape, jnp.float32) * s

    return {
        "x_prompt": nrm(ks[0], (BATCH, SEQ, D_MODEL), 1.0),
        "x_sample": nrm(ks[1], (DEC_BATCH, DEC_SEQ, D_MODEL), 1.0),
        "cache_a_k": nrm(ks[2], (DEC_BATCH, N_ATTN_LAYERS, PAST_LEN, A_KV_HEADS, HEAD_DIM), 1.0),
        "cache_a_v": nrm(ks[3], (DEC_BATCH, N_ATTN_LAYERS, PAST_LEN, A_KV_HEADS, HEAD_DIM), 1.0),
        "cache_b_k": nrm(ks[4], (DEC_BATCH, N_ATTN_LAYERS, PAST_LEN, B_HEADS, HEAD_DIM), 1.0),
        "cache_b_v": nrm(ks[5], (DEC_BATCH, N_ATTN_LAYERS, PAST_LEN, B_HEADS, HEAD_DIM), 1.0),
        "c": nrm(ks[6], (DEC_BATCH, D_MODEL), 1.0),
        "c_ctx": nrm(ks[7], (D_MODEL,), 1.0),
        "ada_w": nrm(ks[8], (DEPTH, D_MODEL, 3 * D_MODEL), 0.5 * D_MODEL ** -0.5),
        "ada_b": nrm(ks[9], (DEPTH, 3 * D_MODEL), 0.02),
        "norm_g": 1.0 + nrm(ks[10], (DEPTH, D_MODEL), 0.05),
        "attn_w_in": nrm(ks[11], (N_ATTN_LAYERS, D_MODEL, ATTN_IN), D_MODEL ** -0.5),
        "attn_w_out": nrm(ks[12], (N_ATTN_LAYERS, ATTN_WIDTH, D_MODEL), ATTN_WIDTH ** -0.5),
        "a_sink": nrm(ks[13], (N_ATTN_LAYERS, A_HEADS), 0.5),
        "b_rpb": nrm(ks[14], (N_ATTN_LAYERS, B_HEADS, 2 * NB_KH - 1, 2 * NB_KW - 1), 0.5),
        "conv_w_in": nrm(ks[15], (N_CONV_LAYERS, D_MODEL, 3 * CONV_WIDTH), D_MODEL ** -0.5),
        "conv_dw": nrm(ks[16], (N_CONV_LAYERS, CONV_K, CONV_WIDTH), CONV_K ** -0.5),
        "conv_dw_b": nrm(ks[17], (N_CONV_LAYERS, CONV_WIDTH), 0.02),
        "conv_ln_g": 1.0 + nrm(ks[18], (N_CONV_LAYERS, CONV_WIDTH), 0.05),
        "conv_ln_b": nrm(ks[19], (N_CONV_LAYERS, CONV_WIDTH), 0.02),
        "conv_w_out": nrm(ks[20], (N_CONV_LAYERS, CONV_WIDTH, D_MODEL), CONV_WIDTH ** -0.5),
        "final_g": 1.0 + nrm(ks[21], (D_MODEL,), 0.05),
    }


def reference(x_prompt, x_sample, cache_a_k, cache_a_v, cache_b_k, cache_b_v, c, c_ctx,
              ada_w, ada_b, norm_g, attn_w_in, attn_w_out, a_sink, b_rpb,
              conv_w_in, conv_dw, conv_dw_b, conv_ln_g, conv_ln_b, conv_w_out, final_g):
    x = x_prompt
    ctx_a_k, ctx_a_v, ctx_b_k, ctx_b_v = [], [], [], []
    for layer in range(DEPTH):
        shift, scale, gate = modulation(c_ctx[None, :], ada_w[layer], ada_b[layer])
        h = rms_norm(x, norm_g[layer]) * (1 + scale) + shift
        if layer % 2 == 0:
            ai = layer // 2
            out, ka, va, kb, vb = attn_mixer_context(h, attn_w_in[ai], attn_w_out[ai], a_sink[ai])
            ctx_a_k.append(ka)
            ctx_a_v.append(va)
            ctx_b_k.append(kb)
            ctx_b_v.append(vb)
        else:
            ci = layer // 2
            out = conv_mixer(h, conv_w_in[ci], conv_dw[ci], conv_dw_b[ci], conv_ln_g[ci],
                             conv_ln_b[ci], conv_w_out[ci])
        x = x + gate * out
    y_prompt = rms_norm(x, final_g)
    new_a_k = jnp.stack(ctx_a_k, axis=1)
    new_a_v = jnp.stack(ctx_a_v, axis=1)
    new_b_k = jnp.stack(ctx_b_k, axis=1)
    new_b_v = jnp.stack(ctx_b_v, axis=1)

    x = x_sample
    cos, sin = axial_rope_tables(x_sample.shape[1], x_sample.dtype)
    for layer in range(DEPTH):
        shift, scale, gate = modulation(c, ada_w[layer], ada_b[layer])
        h = rms_norm(x, norm_g[layer]) * (1 + scale) + shift
        if layer % 2 == 0:
            ai = layer // 2
            out = attn_mixer_latent(h, cache_a_k[:, ai], cache_a_v[:, ai], cache_b_k[:, ai], cache_b_v[:, ai],
                                    attn_w_in[ai], attn_w_out[ai], a_sink[ai], b_rpb[ai], cos, sin)
        else:
            ci = layer // 2
            out = conv_mixer(h, conv_w_in[ci], conv_dw[ci], conv_dw_b[ci], conv_ln_g[ci],
                             conv_ln_b[ci], conv_w_out[ci])
        x = x + gate * out
    y_sample = rms_norm(x, final_g)
    return (y_prompt, y_sample, new_a_k, new_a_v, new_b_k, new_b_v)
```

```python
import functools

import jax
import jax.numpy as jnp
from jax import lax
from jax.experimental import pallas as pl
from jax.experimental.pallas import tpu as pltpu

F32 = jnp.float32
BF16 = jnp.bfloat16

D_MODEL = 2048
DEPTH = 4
GRID_W = 64
HEAD_DIM = 128
A_HEADS = 8
A_KV_HEADS = 2
A_GROUPS = 4
A_BLOCK = 128
B_HEADS = 8
NB_KH = 8
NB_KW = 16
A_WIDTH = A_HEADS * HEAD_DIM
A_KV_WIDTH = A_KV_HEADS * HEAD_DIM
B_WIDTH = B_HEADS * HEAD_DIM
QKV_WIDTH = A_WIDTH + 2 * A_KV_WIDTH + 3 * B_WIDTH
ATTN_IN = QKV_WIDTH + D_MODEL
CONV_K = 31
CONV_HALO = 16
ROPE_BASE = 10000.0
EPS = 1e-6
SCALE = HEAD_DIM ** -0.5
NEG = -1e30

QA_COL = 0
KA_COL = A_WIDTH // HEAD_DIM
VA_COL = KA_COL + A_KV_HEADS
QB_COL = VA_COL + A_KV_HEADS
KB_COL = QB_COL + B_HEADS
VB_COL = KB_COL + B_HEADS

NB_QROWS = 2
NB_WROWS = 10
NB_Q = NB_QROWS * GRID_W
NB_K = NB_WROWS * GRID_W

V7X_VMEM_LIMIT = 56 * 1024 * 1024

IN_TN = 512
PRO_CHUNK = 64


def _silu(x):
    return x * jax.nn.sigmoid(x)


def _params(sem):
    return pltpu.CompilerParams(dimension_semantics=sem, vmem_limit_bytes=V7X_VMEM_LIMIT)


def _mod_kernel(cond_ref, w_ref, b_ref, o_ref):
    s = _silu(cond_ref[...]).astype(BF16)
    o_ref[0] = jnp.dot(s, w_ref[0].astype(BF16), preferred_element_type=F32) + b_ref[0]


def _modulation(cond, ada_w, ada_b):
    rows = cond.shape[0]
    tn = 512
    return pl.pallas_call(
        _mod_kernel,
        grid=(DEPTH, 3 * D_MODEL // tn),
        in_specs=[pl.BlockSpec((rows, D_MODEL), lambda l, j: (0, 0)),
                  pl.BlockSpec((1, D_MODEL, tn), lambda l, j: (l, 0, j)),
                  pl.BlockSpec((1, 1, tn), lambda l, j: (l, 0, j))],
        out_specs=pl.BlockSpec((1, rows, tn), lambda l, j: (l, 0, j)),
        out_shape=jax.ShapeDtypeStruct((DEPTH, rows, 3 * D_MODEL), F32),
        compiler_params=_params(("parallel", "parallel")),
        name="modulation",
    )(cond, ada_w, ada_b.reshape(DEPTH, 1, 3 * D_MODEL))


def _norm_mod_prologue(x_ref, g_ref, mod_ref, h_ref, tm):
    g = g_ref[...]
    shift = mod_ref[0, :, 0:D_MODEL]
    scale1 = 1.0 + mod_ref[0, :, D_MODEL:2 * D_MODEL]

    def body(c, carry):
        r = pl.multiple_of(c * PRO_CHUNK, PRO_CHUNK)
        x = x_ref[pl.ds(r, PRO_CHUNK), :]
        ms = jnp.mean(x * x, axis=-1, keepdims=True)
        y = x * lax.rsqrt(ms + EPS) * g
        h_ref[pl.ds(r, PRO_CHUNK), :] = (y * scale1 + shift).astype(BF16)
        return carry

    lax.fori_loop(0, tm // PRO_CHUNK, body, 0)


def _rope_head(xh, cos, sinp, sinn):
    return xh * cos + pltpu.roll(xh, 32, 1) * sinp + pltpu.roll(xh, 96, 1) * sinn


def _attn_in_kernel(*refs, rope, kv_out, tm):
    x_ref, g_ref, mod_ref, w_ref = refs[:4]
    refs = refs[4:]
    if rope:
        cos_ref, sinp_ref, sinn_ref = refs[:3]
        refs = refs[3:]
    if kv_out:
        qkv_ref, z_ref, kva_ref, kvb_ref, h_ref = refs
    else:
        qkv_ref, z_ref, h_ref = refs
    j = pl.program_id(1)
    n_qkv = QKV_WIDTH // IN_TN
    heads_per_tile = IN_TN // HEAD_DIM

    @pl.when(j == 0)
    def _():
        _norm_mod_prologue(x_ref, g_ref, mod_ref, h_ref, tm)

    acc = jnp.dot(h_ref[...], w_ref[...], preferred_element_type=F32)

    def store_roped(n_heads):
        cos, sinp, sinn = cos_ref[...], sinp_ref[...], sinn_ref[...]
        for c in range(heads_per_tile):
            xh = acc[:, c * HEAD_DIM:(c + 1) * HEAD_DIM]
            if c < n_heads:
                xh = _rope_head(xh, cos, sinp, sinn)
            qkv_ref[:, c * HEAD_DIM:(c + 1) * HEAD_DIM] = xh.astype(BF16)

    if rope:
        qa_tiles = A_WIDTH // IN_TN

        @pl.when(j < qa_tiles)
        def _():
            store_roped(heads_per_tile)

        @pl.when(j == qa_tiles)
        def _():
            store_roped(A_KV_HEADS)

        @pl.when((j > qa_tiles) & (j < n_qkv))
        def _():
            qkv_ref[...] = acc.astype(BF16)
    else:
        @pl.when(j < n_qkv)
        def _():
            qkv_ref[...] = acc.astype(BF16)

    @pl.when(j >= n_qkv)
    def _():
        z_ref[...] = acc.astype(BF16)

    if kv_out:
        @pl.when(j == A_WIDTH // IN_TN)
        def _():
            kva_ref[...] = acc

        kb_tile = (A_WIDTH + 2 * A_KV_WIDTH + B_WIDTH) // IN_TN

        @pl.when((j >= kb_tile) & (j < n_qkv))
        def _():
            kvb_ref[...] = acc


def _attn_in_proj(x, g, mod, w, rope_tabs, rows_per_mod, kv_out):
    m = x.shape[0]
    tm = 1024
    n_qkv = QKV_WIDTH // IN_TN
    grid = (m // tm, ATTN_IN // IN_TN)
    rope = rope_tabs is not None
    in_specs = [pl.BlockSpec((tm, D_MODEL), lambda i, j: (i, 0)),
                pl.BlockSpec((1, D_MODEL), lambda i, j: (0, 0)),
                pl.BlockSpec((1, 1, 3 * D_MODEL), lambda i, j: ((i * tm) // rows_per_mod, 0, 0)),
                pl.BlockSpec((D_MODEL, IN_TN), lambda i, j: (0, j))]
    args = [x, g, mod, w]
    if rope:
        tiles_per_seq = rope_tabs[0].shape[0] // tm
        for t in rope_tabs:
            in_specs.append(pl.BlockSpec((tm, HEAD_DIM), lambda i, j: (i % tiles_per_seq, 0)))
            args.append(t)
    out_specs = [pl.BlockSpec((tm, IN_TN), lambda i, j: (i, jnp.minimum(j, n_qkv - 1))),
                 pl.BlockSpec((tm, IN_TN), lambda i, j: (i, jnp.maximum(j - n_qkv, 0)))]
    out_shape = [jax.ShapeDtypeStruct((m, QKV_WIDTH), BF16), jax.ShapeDtypeStruct((m, D_MODEL), BF16)]
    if kv_out:
        kb_tile = (A_WIDTH + 2 * A_KV_WIDTH + B_WIDTH) // IN_TN
        out_specs += [pl.BlockSpec((tm, IN_TN), lambda i, j: (i, 0)),
                      pl.BlockSpec((tm, IN_TN), lambda i, j: (i, jnp.clip(j - kb_tile, 0, 2 * B_WIDTH // IN_TN - 1)))]
        out_shape += [jax.ShapeDtypeStruct((m, 2 * A_KV_WIDTH), F32), jax.ShapeDtypeStruct((m, 2 * B_WIDTH), F32)]
    return pl.pallas_call(
        functools.partial(_attn_in_kernel, rope=rope, kv_out=kv_out, tm=tm),
        grid=grid, in_specs=in_specs, out_specs=out_specs, out_shape=out_shape,
        scratch_shapes=[pltpu.VMEM((tm, D_MODEL), BF16)],
        compiler_params=_params(("parallel", "arbitrary")),
        name="attn_in_proj",
    )(*args)


def _conv_in_kernel(x_ref, g_ref, mod_ref, wa_ref, wg_ref, wz_ref, u_ref, z_ref, h_ref, *, tm):
    @pl.when(pl.program_id(1) == 0)
    def _():
        _norm_mod_prologue(x_ref, g_ref, mod_ref, h_ref, tm)

    h = h_ref[...]
    a = jnp.dot(h, wa_ref[...], preferred_element_type=F32)
    gl = jnp.dot(h, wg_ref[...], preferred_element_type=F32)
    u_ref[...] = (a * jax.nn.sigmoid(gl)).astype(BF16)
    z_ref[...] = jnp.dot(h, wz_ref[...], preferred_element_type=F32).astype(BF16)


def _conv_in_proj(x, g, mod, w, rows_per_mod):
    m = x.shape[0]
    tm = 1024
    nj = D_MODEL // IN_TN
    wspec = lambda off: pl.BlockSpec((D_MODEL, IN_TN), lambda i, j: (0, j + off * nj))
    return pl.pallas_call(
        functools.partial(_conv_in_kernel, tm=tm),
        grid=(m // tm, nj),
        in_specs=[pl.BlockSpec((tm, D_MODEL), lambda i, j: (i, 0)),
                  pl.BlockSpec((1, D_MODEL), lambda i, j: (0, 0)),
                  pl.BlockSpec((1, 1, 3 * D_MODEL), lambda i, j: ((i * tm) // rows_per_mod, 0, 0)),
                  wspec(0), wspec(1), wspec(2)],
        out_specs=[pl.BlockSpec((tm, IN_TN), lambda i, j: (i, j)),
                   pl.BlockSpec((tm, IN_TN), lambda i, j: (i, j))],
        out_shape=[jax.ShapeDtypeStruct((m, D_MODEL), BF16), jax.ShapeDtypeStruct((m, D_MODEL), BF16)],
        scratch_shapes=[pltpu.VMEM((tm, D_MODEL), BF16)],
        compiler_params=_params(("parallel", "arbitrary")),
        name="conv_in_proj",
    )(x, g, mod, w, w, w)


def _out_proj_kernel(*refs, final):
    ya_ref, yb_ref, w_ref, x_ref, mod_ref = refs[:5]
    if final:
        fg_ref, o_ref = refs[5:]
    else:
        (o_ref,) = refs[5:]
    half = D_MODEL // 2
    tn = 512
    ya = ya_ref[...]
    yb = yb_ref[...]
    ssq = None
    for c in range(D_MODEL // tn):
        cs = slice(c * tn, (c + 1) * tn)
        acc = (jnp.dot(ya, w_ref[0:half, cs], preferred_element_type=F32)
               + jnp.dot(yb, w_ref[half:D_MODEL, cs], preferred_element_type=F32))
        gate = mod_ref[0, :, 2 * D_MODEL + c * tn:2 * D_MODEL + (c + 1) * tn]
        xn = x_ref[:, cs] + gate * acc
        o_ref[:, cs] = xn
        if final:
            part = jnp.sum(xn * xn, axis=-1, keepdims=True)
            ssq = part if ssq is None else ssq + part
    if final:
        r = lax.rsqrt(ssq * (1.0 / D_MODEL) + EPS)
        for c in range(D_MODEL // tn):
            cs = slice(c * tn, (c + 1) * tn)
            o_ref[:, cs] = o_ref[:, cs] * r * fg_ref[:, cs]


def _out_proj(ya, yb, col_b, w, x, mod, rows_per_mod, final_g):
    m = x.shape[0]
    tm = 512
    half = D_MODEL // 2
    final = final_g is not None
    in_specs = [pl.BlockSpec((tm, half), lambda i: (i, 0)),
                pl.BlockSpec((tm, half), lambda i: (i, col_b)),
                pl.BlockSpec((D_MODEL, D_MODEL), lambda i: (0, 0)),
                pl.BlockSpec((tm, D_MODEL), lambda i: (i, 0)),
                pl.BlockSpec((1, 1, 3 * D_MODEL), lambda i: ((i * tm) // rows_per_mod, 0, 0))]
    args = [ya, yb, w, x, mod]
    if final:
        in_specs.append(pl.BlockSpec((1, D_MODEL), lambda i: (0, 0)))
        args.append(final_g)
    return pl.pallas_call(
        functools.partial(_out_proj_kernel, final=final),
        grid=(m // tm,), in_specs=in_specs,
        out_specs=pl.BlockSpec((tm, D_MODEL), lambda i: (i, 0)),
        out_shape=jax.ShapeDtypeStruct((m, D_MODEL), F32),
        compiler_params=_params(("parallel",)),
        name="out_proj",
    )(*args)


def _dot_nt(a, b):
    return lax.dot_general(a, b, (((1,), (1,)), ((), ())), preferred_element_type=F32)


def _softmax_pv(scores, values, sink):
    mx = None
    for s in scores:
        cur = jnp.max(s, axis=-1, keepdims=True)
        mx = cur if mx is None else jnp.maximum(mx, cur)
    if sink is not None:
        mx = jnp.maximum(mx, sink)
    den = None
    out = None
    for s, v in zip(scores, values):
        p = jnp.exp(s - mx)
        d = jnp.sum(p, axis=-1, keepdims=True)
        den = d if den is None else den + d
        o = jnp.dot(p.astype(BF16), v, preferred_element_type=F32)
        out = o if out is None else out + o
    if sink is not None:
        den = den + jnp.exp(sink - mx)
    return out * (1.0 / den)


def _gated_store(y_ref, rows, col, o, z):
    y_ref[rows, col * HEAD_DIM:(col + 1) * HEAD_DIM] = (o * _silu(z.astype(F32))).astype(BF16)


def _head(ref, rows, col):
    return ref[rows, col * HEAD_DIM:(col + 1) * HEAD_DIM]


def _ctx_attn_kernel(sink_ref, qkv_ref, z_ref, ya_ref, yb_ref):
    allrows = slice(None)
    lc = qkv_ref.shape[0]
    for kv in range(A_KV_HEADS):
        k = _head(qkv_ref, allrows, KA_COL + kv)
        v = _head(qkv_ref, allrows, VA_COL + kv)
        q = jnp.concatenate([_head(qkv_ref, allrows, QA_COL + kv * A_GROUPS + g) for g in range(A_GROUPS)], axis=0)
        s_all = _dot_nt(q, k) * SCALE
        for g in range(A_GROUPS):
            h = kv * A_GROUPS + g
            o = _softmax_pv([s_all[g * lc:(g + 1) * lc]], [v], sink_ref[h])
            _gated_store(ya_ref, allrows, h, o, _head(z_ref, allrows, h))
    for h in range(B_HEADS):
        q = _head(qkv_ref, allrows, QB_COL + h)
        k = _head(qkv_ref, allrows, KB_COL + h)
        v = _head(qkv_ref, allrows, VB_COL + h)
        o = _softmax_pv([_dot_nt(q, k) * SCALE], [v], None)
        _gated_store(yb_ref, allrows, h, o, _head(z_ref, allrows, A_HEADS + h))


def _ctx_attention(qkv, z, sink, seq):
    m = qkv.shape[0]
    half = D_MODEL // 2
    return pl.pallas_call(
        _ctx_attn_kernel,
        grid=(m // seq,),
        in_specs=[pl.BlockSpec(memory_space=pltpu.SMEM),
                  pl.BlockSpec((seq, QKV_WIDTH), lambda b: (b, 0)),
                  pl.BlockSpec((seq, D_MODEL), lambda b: (b, 0))],
        out_specs=[pl.BlockSpec((seq, half), lambda b: (b, 0)),
                   pl.BlockSpec((seq, half), lambda b: (b, 0))],
        out_shape=[jax.ShapeDtypeStruct((m, half), BF16), jax.ShapeDtypeStruct((m, half), BF16)],
        compiler_params=_params(("parallel",)),
        name="ctx_attention",
    )(sink, qkv, z)


def _win_attn_kernel(sink_ref, q_ref, k_ref, v_ref, kc_ref, vc_ref, z_ref, y_ref, *, length):
    kv = pl.program_id(1)
    nb = length // A_BLOCK
    win = 3 * A_BLOCK
    kc = kc_ref[0, 0]
    vc = vc_ref[0, 0]
    row = lax.broadcasted_iota(jnp.int32, (A_GROUPS * A_BLOCK, 1), 0)
    sink = jnp.full((A_GROUPS * A_BLOCK, 1), 0.0, F32)
    for g in range(A_GROUPS):
        sink = jnp.where(row // A_BLOCK == g, sink_ref[kv * A_GROUPS + g], sink)
    q_off = row % A_BLOCK
    k_off = lax.broadcasted_iota(jnp.int32, (1, win), 1)

    def body(n, carry):
        r0 = pl.multiple_of(n * A_BLOCK, A_BLOCK)
        ks = pl.multiple_of(jnp.clip(n - 1, 0, nb - 3) * A_BLOCK, A_BLOCK)
        rows = pl.ds(r0, A_BLOCK)
        q = jnp.concatenate([_head(q_ref, rows, g) for g in range(A_GROUPS)], axis=0)
        kl = k_ref[pl.ds(ks, win), :]
        vl = v_ref[pl.ds(ks, win), :]
        mask = jnp.abs((ks + k_off) - (r0 + q_off)) <= A_BLOCK
        s_loc = jnp.where(mask, _dot_nt(q, kl) * SCALE, NEG)
        s_ctx = _dot_nt(q, kc) * SCALE
        o = _softmax_pv([s_loc, s_ctx], [vl, vc], sink)
        for g in range(A_GROUPS):
            _gated_store(y_ref, rows, g, o[g * A_BLOCK:(g + 1) * A_BLOCK], _head(z_ref, rows, g))
        return carry

    lax.fori_loop(0, nb, body, 0)


def _win_attention(qkv, z, kc, vc, sink, length):
    nbatch = qkv.shape[0] // length
    lc = kc.shape[2]
    gw = A_GROUPS * HEAD_DIM
    return pl.pallas_call(
        functools.partial(_win_attn_kernel, length=length),
        grid=(nbatch, A_KV_HEADS),
        in_specs=[pl.BlockSpec(memory_space=pltpu.SMEM),
                  pl.BlockSpec((length, gw), lambda b, kv: (b, kv)),
                  pl.BlockSpec((length, HEAD_DIM), lambda b, kv: (b, KA_COL + kv)),
                  pl.BlockSpec((length, HEAD_DIM), lambda b, kv: (b, VA_COL + kv)),
                  pl.BlockSpec((1, 1, lc, HEAD_DIM), lambda b, kv: (b, kv, 0, 0)),
                  pl.BlockSpec((1, 1, lc, HEAD_DIM), lambda b, kv: (b, kv, 0, 0)),
                  pl.BlockSpec((length, gw), lambda b, kv: (b, kv))],
        out_specs=pl.BlockSpec((length, gw), lambda b, kv: (b, kv)),
        out_shape=jax.ShapeDtypeStruct((qkv.shape[0], A_WIDTH), BF16),
        compiler_params=_params(("parallel", "parallel")),
        name="win_attention",
    )(sink, qkv, qkv, qkv, kc, vc, z)


def _nb_geometry(rows):
    wr = min(NB_KH, rows)
    pats = {}
    for r0 in range(0, rows, NB_QROWS):
        ks = min(max(r0 - NB_KH // 2, 0), rows - NB_WROWS)
        o = r0 - ks
        pat = []
        for a in range(NB_QROWS):
            r = r0 + a
            r_start = min(max(r - NB_KH // 2, 0), rows - wr)
            pat.append([(ks + i - r + NB_KH - 1) if r_start <= ks + i < r_start + wr else None
                        for i in range(NB_WROWS)])
        assert pats.setdefault(o, pat) == pat
    return pats


def _nb_bias_kernel(rpb_ref, o_ref, *, pats):
    h = pl.program_id(0)
    n_r, n_c = 2 * NB_KH - 1, 2 * NB_KW - 1
    qc = lax.broadcasted_iota(jnp.int32, (GRID_W, 2 * GRID_W), 0)
    lane = lax.broadcasted_iota(jnp.int32, (GRID_W, 2 * GRID_W), 1)
    kc = lane % GRID_W
    cidx = jnp.clip(kc - qc + NB_KW - 1, 0, n_c - 1)
    cstart = jnp.clip(qc - NB_KW // 2, 0, GRID_W - NB_KW)
    col_ok = (kc >= cstart) & (kc < cstart + NB_KW)
    masked = jnp.full((GRID_W, 2 * GRID_W), NEG, F32)
    toeplitz = []
    for j in range(n_r):
        acc = jnp.zeros((GRID_W, 2 * GRID_W), F32)
        for c in range(n_c):
            acc = jnp.where(cidx == c, rpb_ref[h * (n_r * n_c) + j * n_c + c], acc)
        toeplitz.append(jnp.where(col_ok, acc, NEG))
    for p, o in enumerate(sorted(pats)):
        for a in range(NB_QROWS):
            row_idx = pats[o][a]
            for mth in range(NB_WROWS // 2):
                j0, j1 = row_idx[2 * mth], row_idx[2 * mth + 1]
                t0 = masked if j0 is None else toeplitz[j0]
                t1 = masked if j1 is None else toeplitz[j1]
                o_ref[0, p, a * GRID_W:(a + 1) * GRID_W, mth * 2 * GRID_W:(mth + 1) * 2 * GRID_W] = (
                    jnp.where(lane < GRID_W, t0, t1))


def _nb_bias_table(rpb, rows):
    pats = _nb_geometry(rows)
    return pl.pallas_call(
        functools.partial(_nb_bias_kernel, pats=pats),
        grid=(B_HEADS,),
        in_specs=[pl.BlockSpec(memory_space=pltpu.SMEM)],
        out_specs=pl.BlockSpec((1, len(pats), NB_Q, NB_K), lambda h: (h, 0, 0, 0)),
        out_shape=jax.ShapeDtypeStruct((B_HEADS, len(pats), NB_Q, NB_K), F32),
        compiler_params=_params(("parallel",)),
        name="nb_bias_table",
    )(rpb.reshape(-1))


def _nb_attn_kernel(q_ref, k_ref, v_ref, kc_ref, vc_ref, z_ref, bias_ref, y_ref, *, rows):
    kc = kc_ref[0, 0]
    vc = vc_ref[0, 0]

    def body(rb, carry):
        r0 = rb * NB_QROWS
        ks = jnp.clip(r0 - NB_KH // 2, 0, rows - NB_WROWS)
        pid = (r0 - ks) // NB_QROWS
        qrows = pl.ds(pl.multiple_of(rb * NB_Q, NB_Q), NB_Q)
        krows = pl.ds(pl.multiple_of(ks * GRID_W, GRID_W), NB_K)
        q = q_ref[qrows, :]
        s_loc = _dot_nt(q, k_ref[krows, :]) * SCALE + bias_ref[0, pid]
        s_ctx = _dot_nt(q, kc) * SCALE
        o = _softmax_pv([s_loc, s_ctx], [v_ref[krows, :], vc], None)
        _gated_store(y_ref, qrows, 0, o, z_ref[qrows, :])
        return carry

    lax.fori_loop(0, rows // NB_QROWS, body, 0)


def _nb_attention(qkv, z, kc, vc, bias, length):
    nbatch = qkv.shape[0] // length
    lc = kc.shape[2]
    rows = length // GRID_W
    head = lambda col: pl.BlockSpec((length, HEAD_DIM), lambda h, b: (b, col + h))
    return pl.pallas_call(
        functools.partial(_nb_attn_kernel, rows=rows),
        grid=(B_HEADS, nbatch),
        in_specs=[head(QB_COL), head(KB_COL), head(VB_COL),
                  pl.BlockSpec((1, 1, lc, HEAD_DIM), lambda h, b: (b, h, 0, 0)),
                  pl.BlockSpec((1, 1, lc, HEAD_DIM), lambda h, b: (b, h, 0, 0)),
                  head(A_HEADS),
                  pl.BlockSpec((1,) + bias.shape[1:], lambda h, b: (h, 0, 0, 0))],
        out_specs=head(0),
        out_shape=jax.ShapeDtypeStruct((qkv.shape[0], B_WIDTH), BF16),
        compiler_params=_params(("parallel", "parallel")),
        name="nb_attention",
    )(qkv, qkv, qkv, kc, vc, z, bias)


CONV_ROWS = 64
CONV_LANES = 512


def _conv_kernel(uc_ref, up_ref, un_ref, z_ref, dw_ref, dwb_ref, lg_ref, lb_ref, y_ref,
                 xp_ref, wb_ref, cb_ref, *, tile, tiles_per_seq):
    i = pl.program_id(0)

    @pl.when(i == 0)
    def _():
        for k in range(CONV_K):
            wb_ref[k] = jnp.broadcast_to(dw_ref[k:k + 1, :], (8, D_MODEL))

    first = (i % tiles_per_seq) == 0
    last = (i % tiles_per_seq) == tiles_per_seq - 1
    xp_ref[0:CONV_HALO, :] = jnp.where(first, 0.0, up_ref[...].astype(F32))
    xp_ref[CONV_HALO:CONV_HALO + tile, :] = uc_ref[...].astype(F32)
    xp_ref[CONV_HALO + tile:2 * CONV_HALO + tile, :] = jnp.where(last, 0.0, un_ref[...].astype(F32))
    lead = CONV_HALO - CONV_K // 2

    def body(rc, carry):
        r0 = rc * CONV_ROWS
        for c in range(D_MODEL // CONV_LANES):
            ls = slice(c * CONV_LANES, (c + 1) * CONV_LANES)
            acc = jnp.zeros((CONV_ROWS // 8, 8, CONV_LANES), F32)
            for k in range(CONV_K):
                xs = xp_ref[pl.ds(r0 + (lead + k), CONV_ROWS), ls]
                acc = acc + xs.reshape(CONV_ROWS // 8, 8, CONV_LANES) * wb_ref[k, :, ls][None]
            cb_ref[:, ls] = acc.reshape(CONV_ROWS, CONV_LANES) + dwb_ref[:, ls]
        cv = cb_ref[...]
        mu = jnp.mean(cv, axis=-1, keepdims=True)
        xc = cv - mu
        var = jnp.mean(xc * xc, axis=-1, keepdims=True)
        ln = xc * lax.rsqrt(var + EPS) * lg_ref[...] + lb_ref[...]
        zz = z_ref[pl.ds(r0, CONV_ROWS), :].astype(F32)
        y_ref[pl.ds(r0, CONV_ROWS), :] = (_silu(ln) * _silu(zz)).astype(BF16)
        return carry

    for rc in range(tile // CONV_ROWS):
        body(rc, 0)


def _conv_module(u, z, dw, dw_b, ln_g, ln_b, seq):
    m = u.shape[0]
    tile = 256
    tiles_per_seq = seq // tile
    hb = tile // CONV_HALO
    row = lambda a: a.reshape(1, D_MODEL)
    vec = pl.BlockSpec((1, D_MODEL), lambda i: (0, 0))
    return pl.pallas_call(
        functools.partial(_conv_kernel, tile=tile, tiles_per_seq=tiles_per_seq),
        grid=(m // tile,),
        in_specs=[pl.BlockSpec((tile, D_MODEL), lambda i: (i, 0)),
                  pl.BlockSpec((CONV_HALO, D_MODEL), lambda i: (jnp.maximum(i * hb - 1, 0), 0)),
                  pl.BlockSpec((CONV_HALO, D_MODEL), lambda i: (jnp.minimum((i + 1) * hb, m // CONV_HALO - 1), 0)),
                  pl.BlockSpec((tile, D_MODEL), lambda i: (i, 0)),
                  pl.BlockSpec((CONV_K, D_MODEL), lambda i: (0, 0)),
                  vec, vec, vec],
        out_specs=pl.BlockSpec((tile, D_MODEL), lambda i: (i, 0)),
        out_shape=jax.ShapeDtypeStruct((m, D_MODEL), BF16),
        scratch_shapes=[pltpu.VMEM((tile + 2 * CONV_HALO, D_MODEL), F32),
                        pltpu.VMEM((CONV_K, 8, D_MODEL), F32),
                        pltpu.VMEM((CONV_ROWS, D_MODEL), F32)],
        compiler_params=_params(("arbitrary",)),
        name="conv_module",
    )(u, u, u, z, dw, row(dw_b), row(ln_g), row(ln_b))


def _rope_tables(length):
    t = jnp.arange(length)
    n_freq = HEAD_DIM // 4
    inv = ROPE_BASE ** (-jnp.arange(n_freq, dtype=F32) / n_freq)
    ang_r = (t // GRID_W).astype(F32)[:, None] * inv[None, :]
    ang_c = (t % GRID_W).astype(F32)[:, None] * inv[None, :]
    ang = jnp.concatenate([ang_r, ang_r, ang_c, ang_c], axis=-1)
    cos, sin = jnp.cos(ang), jnp.sin(ang)
    upper = (jnp.arange(HEAD_DIM) % (2 * n_freq)) >= n_freq
    return cos, jnp.where(upper, sin, 0.0), jnp.where(upper, 0.0, -sin)


def _trunk(x3, mod_rows, weights, latent_ctx):
    (norm_g, w_in_a, w_out_a, a_sink, b_rpb, w_in_c, conv_dw, conv_dw_b, conv_ln_g, conv_ln_b, w_out_c,
     final_g) = weights
    n, seq, _ = x3.shape
    m = n * seq
    x = x3.reshape(m, D_MODEL)
    rows_per_mod = m if mod_rows.shape[1] == 1 else seq
    half = D_MODEL // 2
    new_kv = []
    for layer in range(DEPTH):
        mod = mod_rows[layer]
        g = norm_g[layer].reshape(1, D_MODEL)
        fg = final_g.reshape(1, D_MODEL) if layer == DEPTH - 1 else None
        if layer % 2 == 0:
            ai = layer // 2
            if latent_ctx is None:
                qkv, z, kva, kvb = _attn_in_proj(x, g, mod, w_in_a[ai], None, rows_per_mod, True)
                new_kv.append((kva[:, :A_KV_WIDTH].reshape(n, seq, A_KV_HEADS, HEAD_DIM),
                               kva[:, A_KV_WIDTH:].reshape(n, seq, A_KV_HEADS, HEAD_DIM),
                               kvb[:, :B_WIDTH].reshape(n, seq, B_HEADS, HEAD_DIM),
                               kvb[:, B_WIDTH:].reshape(n, seq, B_HEADS, HEAD_DIM)))
                ya, yb = _ctx_attention(qkv, z, a_sink[ai], seq)
            else:
                cak, cav, cbk, cbv, rope_tabs = latent_ctx
                heads_first = lambda t: jnp.transpose(t[:, ai], (0, 2, 1, 3)).astype(BF16)
                qkv, z = _attn_in_proj(x, g, mod, w_in_a[ai], rope_tabs, rows_per_mod, False)
                ya = _win_attention(qkv, z, heads_first(cak), heads_first(cav), a_sink[ai], seq)
                bias = _nb_bias_table(b_rpb[ai], seq // GRID_W)
                yb = _nb_attention(qkv, z, heads_first(cbk), heads_first(cbv), bias, seq)
            x = _out_proj(ya, yb, 0, w_out_a[ai], x, mod, rows_per_mod, fg)
        else:
            ci = layer // 2
            u, z = _conv_in_proj(x, g, mod, w_in_c[ci], rows_per_mod)
            y = _conv_module(u, z, conv_dw[ci], conv_dw_b[ci], conv_ln_g[ci], conv_ln_b[ci], seq)
            x = _out_proj(y, y, 1, w_out_c[ci], x, mod, rows_per_mod, fg)
    return x.reshape(n, seq, D_MODEL), new_kv


def kernel(x_prompt, x_sample, cache_a_k, cache_a_v, cache_b_k, cache_b_v, c, c_ctx, ada_w, ada_b, norm_g,
           attn_w_in, attn_w_out, a_sink, b_rpb, conv_w_in, conv_dw, conv_dw_b, conv_ln_g, conv_ln_b,
           conv_w_out, final_g):
    n_lat = c.shape[0]
    pad = (-(1 + n_lat)) % 8
    cond = jnp.concatenate([c_ctx[None, :], c, jnp.zeros((pad, D_MODEL), F32)], axis=0)
    mod = _modulation(cond, ada_w, ada_b)
    mod_ctx = mod[:, 0:1, None, :]
    mod_lat = mod[:, 1:1 + n_lat, None, :]
    weights = (norm_g, attn_w_in.astype(BF16), attn_w_out.astype(BF16), a_sink, b_rpb, conv_w_in.astype(BF16),
               conv_dw, conv_dw_b, conv_ln_g, conv_ln_b, conv_w_out.astype(BF16), final_g)

    y_prompt, new_kv = _trunk(x_prompt, mod_ctx, weights, None)
    new_a_k, new_a_v, new_b_k, new_b_v = (jnp.stack(t, axis=1) for t in zip(*new_kv))

    rope_tabs = _rope_tables(x_sample.shape[1])
    y_sample, _ = _trunk(x_sample, mod_lat, weights, (cache_a_k, cache_a_v, cache_b_k, cache_b_v, rope_tabs))
    return (y_prompt, y_sample, new_a_k, new_a_v, new_b_k, new_b_v)
```
